```python
import math
import jax, jax.numpy as jnp
from jax import lax
import numpy as np

D_MODEL = 4096
BATCH = 8
SEQ = 2048
DEPTH = 2
DEC_BATCH = 4
DEC_SEQ = 4096
PAST_LEN = 128

GRID_W = 64
HEAD_DIM = 128
Q_BLOCK = 128
RMS_EPS = 1e-6
ROPE_THETA = 10000.0
GROUP_W = D_MODEL // 4
FNET_GROUPS = 8
FNET_GROUP_W = GROUP_W // FNET_GROUPS
DIFF_HEADS = GROUP_W // HEAD_DIM
DIFF_DIM = HEAD_DIM // 2
GQA_HEADS = GROUP_W // HEAD_DIM
GQA_KV_HEADS = 2
GQA_GROUP = GQA_HEADS // GQA_KV_HEADS
NA_HEADS = GROUP_W // HEAD_DIM
NA_KH = 8
NA_KW = 16
N_EXPERTS = 16
EC_CAPACITY = 2
D_FF_EXPERT = D_MODEL
SPLIT_WIDTHS = (
    GROUP_W,
    DIFF_HEADS * 2 * DIFF_DIM,
    DIFF_HEADS * 2 * DIFF_DIM,
    DIFF_HEADS * HEAD_DIM,
    GQA_HEADS * HEAD_DIM,
    GQA_KV_HEADS * HEAD_DIM,
    GQA_KV_HEADS * HEAD_DIM,
    NA_HEADS * HEAD_DIM,
    NA_HEADS * HEAD_DIM,
    NA_HEADS * HEAD_DIM,
)
IN_WIDTH = sum(SPLIT_WIDTHS)

kernel_name = "hybrid_parallel_group_encoder"


def rmsnorm(x, g):
    x32 = x.astype(jnp.float32)
    y = x32 * lax.rsqrt(jnp.mean(x32 * x32, axis=-1, keepdims=True) + RMS_EPS)
    return (y * g.astype(jnp.float32)).astype(x.dtype)


def rope_angles(pos, dim):
    inv = ROPE_THETA ** (-jnp.arange(0, dim, 2, dtype=jnp.float32) / dim)
    return pos.astype(jnp.float32)[:, None] * inv[None, :]


def apply_rope(x, ang):
    x32 = x.astype(jnp.float32)
    half = x.shape[-1] // 2
    x1, x2 = x32[..., :half], x32[..., half:]
    c, s = jnp.cos(ang), jnp.sin(ang)
    return jnp.concatenate([x1 * c - x2 * s, x2 * c + x1 * s], axis=-1).astype(x.dtype)


def axial_rope(x, ang_row, ang_col):
    half = HEAD_DIM // 2
    return jnp.concatenate([apply_rope(x[..., :half], ang_row), apply_rope(x[..., half:], ang_col)], axis=-1)


def fourier_mixer(u):
    B, S, _ = u.shape
    a = u.reshape(B, S, FNET_GROUPS, FNET_GROUP_W).astype(jnp.float32)
    mixed = jnp.fft.fftn(a, axes=(1, 3), norm="ortho").real
    return mixed.reshape(B, S, GROUP_W).astype(u.dtype)


def diff_attention(q, k, v, lam_params, subln, layer_idx):
    B, S, _ = q.shape
    nb = S // Q_BLOCK
    ang = rope_angles(jnp.arange(S), DIFF_DIM)
    q = apply_rope(q.reshape(B, S, DIFF_HEADS, 2, DIFF_DIM).transpose(0, 2, 3, 1, 4), ang)
    k = apply_rope(k.reshape(B, S, DIFF_HEADS, 2, DIFF_DIM).transpose(0, 2, 3, 1, 4), ang)
    v = v.reshape(B, S, DIFF_HEADS, HEAD_DIM).transpose(0, 2, 1, 3)
    lam_init = 0.8 - 0.6 * math.exp(-0.3 * layer_idx)
    lp = lam_params.astype(jnp.float32)
    lam = jnp.exp(jnp.sum(lp[0] * lp[1])) - jnp.exp(jnp.sum(lp[2] * lp[3])) + lam_init
    scale = DIFF_DIM ** -0.5
    qb = jnp.moveaxis(q.reshape(B, DIFF_HEADS, 2, nb, Q_BLOCK, DIFF_DIM), 3, 0)

    def block(qblk):
        s = jnp.einsum('bhiqd,bhikd->bhiqk', qblk, k, preferred_element_type=jnp.float32) * scale
        p = jax.nn.softmax(s, axis=-1)
        p_diff = (p[:, :, 0] - lam * p[:, :, 1]).astype(v.dtype)
        return jnp.einsum('bhqk,bhkv->bhqv', p_diff, v)

    o = lax.map(block, qb)
    o = o.transpose(1, 0, 3, 2, 4).reshape(B, S, DIFF_HEADS, HEAD_DIM)
    o = rmsnorm(o, subln) * (1.0 - lam_init)
    return o.reshape(B, S, GROUP_W)


def gqa_axial(q, k, v, q_norm, k_norm):
    B, S, _ = q.shape
    nb = S // Q_BLOCK
    t = jnp.arange(S)
    ang_r = rope_angles(t // GRID_W, HEAD_DIM // 2)
    ang_c = rope_angles(t % GRID_W, HEAD_DIM // 2)
    q = rmsnorm(q.reshape(B, S, GQA_KV_HEADS, GQA_GROUP, HEAD_DIM), q_norm).transpose(0, 2, 3, 1, 4)
    k = rmsnorm(k.reshape(B, S, GQA_KV_HEADS, HEAD_DIM), k_norm).transpose(0, 2, 1, 3)
    q = axial_rope(q, ang_r, ang_c)
    k = axial_rope(k, ang_r, ang_c)
    v = v.reshape(B, S, GQA_KV_HEADS, HEAD_DIM).transpose(0, 2, 1, 3)
    scale = HEAD_DIM ** -0.5
    qb = jnp.moveaxis(q.reshape(B, GQA_KV_HEADS, GQA_GROUP, nb, Q_BLOCK, HEAD_DIM), 3, 0)

    def block(qblk):
        s = jnp.einsum('bgrqd,bgkd->bgrqk', qblk, k, preferred_element_type=jnp.float32) * scale
        p = jax.nn.softmax(s, axis=-1).astype(v.dtype)
        return jnp.einsum('bgrqk,bgkd->bgrqd', p, v)

    o = lax.map(block, qb)
    return o.transpose(1, 0, 4, 2, 3, 5).reshape(B, S, GROUP_W)


def neighbourhood_attention(q, k, v, rel_bias):
    B, S, _ = q.shape
    R = S // GRID_W
    kh = min(NA_KH, R)

    def grid(u):
        return u.reshape(B, R, GRID_W, NA_HEADS, HEAD_DIM).transpose(0, 3, 1, 2, 4)

    qg, kg, vg = grid(q), grid(k), grid(v)
    r = jnp.arange(R)
    row_start = jnp.clip(r - NA_KH // 2, 0, R - kh)
    row_idx = row_start[:, None] + jnp.arange(kh)[None, :]
    k_rows = kg[:, :, row_idx]
    v_rows = vg[:, :, row_idx]
    c = jnp.arange(GRID_W)
    col_start = jnp.clip(c - NA_KW // 2, 0, GRID_W - NA_KW)
    col_mask = (c[None, :] >= col_start[:, None]) & (c[None, :] < col_start[:, None] + NA_KW)
    row_off = row_idx - r[:, None] + (NA_KH - 1)
    col_off = jnp.clip(c[None, :] - c[:, None] + (NA_KW - 1), 0, 2 * NA_KW - 2)
    bias = rel_bias[:, row_off[:, None, :, None], col_off[None, :, None, :]]
    scale = HEAD_DIM ** -0.5
    s = jnp.einsum('bhrqd,bhrjwd->bhrqjw', qg, k_rows, preferred_element_type=jnp.float32) * scale
    s = s + bias.astype(jnp.float32)[None]
    s = jnp.where(col_mask[:, None, :], s, -jnp.inf)
    p = jax.nn.softmax(s.reshape(B, NA_HEADS, R, GRID_W, kh * GRID_W), axis=-1).reshape(s.shape)
    o = jnp.einsum('bhrqjw,bhrjwd->bhrqd', p.astype(v.dtype), v_rows)
    return o.transpose(0, 2, 3, 1, 4).reshape(B, S, GROUP_W)


def expert_choice_ffn(x, w_router, w_gate, w_up, w_down):
    B, S, D = x.shape
    n = B * S
    x2 = x.reshape(n, D)
    cap = EC_CAPACITY * n // N_EXPERTS
    aff = jax.nn.softmax(jnp.einsum('nd,de->ne', x2, w_router, preferred_element_type=jnp.float32), axis=-1)
    gate, tok = lax.top_k(aff.T, cap)
    xe = x2[tok]
    h = jax.nn.silu(jnp.einsum('ecd,edf->ecf', xe, w_gate)) * jnp.einsum('ecd,edf->ecf', xe, w_up)
    ye = jnp.einsum('ecf,efd->ecd', h, w_down) * gate[..., None].astype(x.dtype)
    y = jnp.zeros_like(x2).at[tok.reshape(-1)].add(ye.reshape(-1, D))
    return y.reshape(B, S, D)


def trunk(x, attn_norm, w_in, diff_lambda, diff_subln, gqa_q_norm, gqa_k_norm, na_rel_bias,
          w_out, ffn_norm, w_router, w_gate, w_up, w_down, final_norm):
    offsets = np.cumsum(SPLIT_WIDTHS)[:-1].tolist()
    for l in range(DEPTH):
        h = rmsnorm(x, attn_norm[l])
        proj = jnp.einsum('bsd,de->bse', h, w_in[l])
        u_a, q_b, k_b, v_b, q_c, k_c, v_c, q_d, k_d, v_d = jnp.split(proj, offsets, axis=-1)
        mixed = jnp.concatenate([
            fourier_mixer(u_a),
            diff_attention(q_b, k_b, v_b, diff_lambda[l], diff_subln[l], l),
            gqa_axial(q_c, k_c, v_c, gqa_q_norm[l], gqa_k_norm[l]),
            neighbourhood_attention(q_d, k_d, v_d, na_rel_bias[l]),
        ], axis=-1)
        x = x + jnp.einsum('bse,ed->bsd', mixed, w_out[l])
        x = x + expert_choice_ffn(rmsnorm(x, ffn_norm[l]), w_router[l], w_gate[l], w_up[l], w_down[l])
    return rmsnorm(x, final_norm)


def setup_inputs(seed: int = 0) -> dict:
    key = jax.random.key(seed)
    ks = jax.random.split(key, 16)

    def normal(k, shape, scale):
        return jax.random.normal(k, shape, jnp.float32) * scale

    return {
        "x_prompt": normal(ks[0], (BATCH, SEQ, D_MODEL), 1.0),
        "x_sample": normal(ks[1], (DEC_BATCH, DEC_SEQ, D_MODEL), 1.0),
        "attn_norm": 1.0 + normal(ks[2], (DEPTH, D_MODEL), 0.02),
        "w_in": normal(ks[3], (DEPTH, D_MODEL, IN_WIDTH), D_MODEL ** -0.5),
        "diff_lambda": normal(ks[4], (DEPTH, 4, DIFF_DIM), 0.1),
        "diff_subln": 1.0 + normal(ks[5], (DEPTH, HEAD_DIM), 0.02),
        "gqa_q_norm": 1.0 + normal(ks[6], (DEPTH, HEAD_DIM), 0.02),
        "gqa_k_norm": 1.0 + normal(ks[7], (DEPTH, HEAD_DIM), 0.02),
        "na_rel_bias": normal(ks[8], (DEPTH, NA_HEADS, 2 * NA_KH - 1, 2 * NA_KW - 1), 0.1),
        "w_out": normal(ks[9], (DEPTH, D_MODEL, D_MODEL), D_MODEL ** -0.5),
        "ffn_norm": 1.0 + normal(ks[10], (DEPTH, D_MODEL), 0.02),
        "w_router": normal(ks[11], (DEPTH, D_MODEL, N_EXPERTS), D_MODEL ** -0.5),
        "w_gate": normal(ks[12], (DEPTH, N_EXPERTS, D_MODEL, D_FF_EXPERT), D_MODEL ** -0.5),
        "w_up": normal(ks[13], (DEPTH, N_EXPERTS, D_MODEL, D_FF_EXPERT), D_MODEL ** -0.5),
        "w_down": normal(ks[14], (DEPTH, N_EXPERTS, D_FF_EXPERT, D_MODEL), D_FF_EXPERT ** -0.5),
        "final_norm": 1.0 + normal(ks[15], (D_MODEL,), 0.02),
    }


def reference(x_prompt, x_sample, attn_norm, w_in, diff_lambda, diff_subln, gqa_q_norm, gqa_k_norm,
              na_rel_bias, w_out, ffn_norm, w_router, w_gate, w_up, w_down, final_norm):
    y_prompt = trunk(x_prompt, attn_norm, w_in, diff_lambda, diff_subln, gqa_q_norm, gqa_k_norm,
                     na_rel_bias, w_out, ffn_norm, w_router, w_gate, w_up, w_down, final_norm)
    y_sample = trunk(x_sample, attn_norm, w_in, diff_lambda, diff_subln, gqa_q_norm, gqa_k_norm,
                     na_rel_bias, w_out, ffn_norm, w_router, w_gate, w_up, w_down, final_norm)
    return (y_prompt, y_sample)
```

```python
import functools
import math

import jax
import jax.numpy as jnp
from jax import lax
from jax.experimental import pallas as pl
from jax.experimental.pallas import tpu as pltpu

D_MODEL = 4096
DEPTH = 2
GRID_W = 64
HEAD_DIM = 128
RMS_EPS = 1e-6
ROPE_THETA = 10000.0
GROUP_W = D_MODEL // 4
FNET_GROUPS = 8
FNET_GROUP_W = GROUP_W // FNET_GROUPS
DIFF_HEADS = GROUP_W // HEAD_DIM
DIFF_DIM = HEAD_DIM // 2
GQA_HEADS = GROUP_W // HEAD_DIM
GQA_KV_HEADS = 2
GQA_GROUP = GQA_HEADS // GQA_KV_HEADS
NA_HEADS = GROUP_W // HEAD_DIM
NA_KH = 8
NA_KW = 16
N_EXPERTS = 16
EC_CAPACITY = 2
D_FF_EXPERT = D_MODEL

OFF_A = 0
OFF_BQ = OFF_A + GROUP_W
OFF_BK = OFF_BQ + GROUP_W
OFF_BV = OFF_BK + GROUP_W
OFF_CQ = OFF_BV + GROUP_W
OFF_CK = OFF_CQ + GROUP_W
OFF_CV = OFF_CK + GQA_KV_HEADS * HEAD_DIM
OFF_DQ = OFF_CV + GQA_KV_HEADS * HEAD_DIM
OFF_DK = OFF_DQ + GROUP_W
OFF_DV = OFF_DK + GROUP_W
IN_WIDTH = OFF_DV + GROUP_W

V7X_LANES = 128
V7X_VMEM_BYTES = 64 * 1024 * 1024
VMEM_LIMIT = V7X_VMEM_BYTES - 8 * 1024 * 1024

MASK_VALUE = -1e30

_NT = (((1,), (1,)), ((), ()))


def _params(n_grid_dims):
    return pltpu.CompilerParams(
        dimension_semantics=("arbitrary",) * n_grid_dims, vmem_limit_bytes=VMEM_LIMIT)


def _rms(x, gain):
    ms = jnp.mean(x * x, axis=-1, keepdims=True)
    return (x * lax.rsqrt(ms + RMS_EPS)) * gain


def _rope(x, cos, sin_up, sin_dn):
    return x * cos + pltpu.roll(x, 96, 1) * sin_up + pltpu.roll(x, 32, 1) * sin_dn


def _rope_tables(ang_lo, ang_hi):
    z = jnp.zeros_like(ang_lo)
    c_lo, s_lo, c_hi, s_hi = jnp.cos(ang_lo), jnp.sin(ang_lo), jnp.cos(ang_hi), jnp.sin(ang_hi)
    cos = jnp.concatenate([c_lo, c_lo, c_hi, c_hi], axis=-1)
    sin_up = jnp.concatenate([-s_lo, z, -s_hi, z], axis=-1)
    sin_dn = jnp.concatenate([z, s_lo, z, s_hi], axis=-1)
    return cos, sin_up, sin_dn


def _rope_angles(pos, dim):
    inv = ROPE_THETA ** (-jnp.arange(0, dim, 2, dtype=jnp.float32) / dim)
    return pos.astype(jnp.float32)[:, None] * inv[None, :]


def _rmsnorm_kernel(x_ref, g_ref, o_ref):
    o_ref[...] = _rms(x_ref[...], g_ref[...]).astype(o_ref.dtype)


def _rmsnorm(x, gain, out_dtype, tm=256):
    n, d = x.shape
    return pl.pallas_call(
        _rmsnorm_kernel,
        grid=(n // tm,),
        in_specs=[pl.BlockSpec((tm, d), lambda i: (i, 0)), pl.BlockSpec((1, d), lambda i: (0, 0))],
        out_specs=pl.BlockSpec((tm, d), lambda i: (i, 0)),
        out_shape=jax.ShapeDtypeStruct((n, d), out_dtype),
        compiler_params=_params(1),
        name="rmsnorm",
    )(x, gain.reshape(1, d))


def _matmul_kernel(a_ref, b_ref, o_ref):
    o_ref[...] = jnp.dot(a_ref[...], b_ref[...], preferred_element_type=jnp.float32).astype(o_ref.dtype)


def _matmul(a, b, out_dtype, tm=1024, tn=512):
    m, k = a.shape
    _, n = b.shape
    tm = min(tm, m)
    return pl.pallas_call(
        _matmul_kernel,
        grid=(m // tm, n // tn),
        in_specs=[pl.BlockSpec((tm, k), lambda i, j: (i, 0)), pl.BlockSpec((k, tn), lambda i, j: (0, j))],
        out_specs=pl.BlockSpec((tm, tn), lambda i, j: (i, j)),
        out_shape=jax.ShapeDtypeStruct((m, n), out_dtype),
        compiler_params=_params(2),
        name="in_proj",
    )(a, b)


def _out_proj_kernel(fa_ref, db_ref, gc_ref, nd_ref, w_ref, x_ref, o_ref):
    acc = x_ref[...]
    for i, a_ref in enumerate((fa_ref, db_ref, gc_ref, nd_ref)):
        acc = acc + jnp.dot(a_ref[...], w_ref[i * GROUP_W:(i + 1) * GROUP_W, :],
                            preferred_element_type=jnp.float32)
    o_ref[...] = acc


def _out_proj(mixed, w, x, tm=1024, tn=512):
    n, d = x.shape
    tm = min(tm, n)
    a_spec = pl.BlockSpec((tm, GROUP_W), lambda i, j: (i, 0))
    return pl.pallas_call(
        _out_proj_kernel,
        grid=(n // tm, d // tn),
        in_specs=[a_spec, a_spec, a_spec, a_spec,
                  pl.BlockSpec((d, tn), lambda i, j: (0, j)),
                  pl.BlockSpec((tm, tn), lambda i, j: (i, j))],
        out_specs=pl.BlockSpec((tm, tn), lambda i, j: (i, j)),
        out_shape=jax.ShapeDtypeStruct((n, d), jnp.float32),
        compiler_params=_params(2),
        name="out_proj",
    )(*mixed, w, x)


def _dft_tables(n):
    idx = jnp.arange(n, dtype=jnp.int32)
    ang = ((idx[:, None] * idx[None, :]) % n).astype(jnp.float32) * (2.0 * math.pi / n)
    return jnp.cos(ang).astype(jnp.bfloat16), (-jnp.sin(ang)).astype(jnp.bfloat16)


def _fourier_kernel(a_ref, c128_ref, ms128_ref, cs_ref, mss_ref, o_ref, ac_ref, as_ref, *, scale):
    @pl.when(pl.program_id(1) == 0)
    def _():
        for g in range(FNET_GROUPS):
            cols = slice(g * FNET_GROUP_W, (g + 1) * FNET_GROUP_W)
            a = a_ref[:, cols]
            ac_ref[:, cols] = jnp.dot(a, c128_ref[...], preferred_element_type=jnp.float32).astype(ac_ref.dtype)
            as_ref[:, cols] = jnp.dot(a, ms128_ref[...], preferred_element_type=jnp.float32).astype(as_ref.dtype)

    o = jnp.dot(cs_ref[...], ac_ref[...], preferred_element_type=jnp.float32)
    o = o - jnp.dot(mss_ref[...], as_ref[...], preferred_element_type=jnp.float32)
    o_ref[...] = (o * scale).astype(o_ref.dtype)


def _fourier(proj, batch, seq, tq=256):
    n = batch * seq
    tq = min(tq, seq)
    cs, mss = _dft_tables(seq)
    c128, ms128 = _dft_tables(FNET_GROUP_W)
    nq = seq // tq
    kern = functools.partial(_fourier_kernel, scale=1.0 / math.sqrt(seq * FNET_GROUP_W))
    return pl.pallas_call(
        kern,
        grid=(batch, nq),
        in_specs=[pl.BlockSpec((seq, GROUP_W), lambda b, i: (b, OFF_A // GROUP_W)),
                  pl.BlockSpec((FNET_GROUP_W, FNET_GROUP_W), lambda b, i: (0, 0)),
                  pl.BlockSpec((FNET_GROUP_W, FNET_GROUP_W), lambda b, i: (0, 0)),
                  pl.BlockSpec((tq, seq), lambda b, i: (i, 0)),
                  pl.BlockSpec((tq, seq), lambda b, i: (i, 0))],
        out_specs=pl.BlockSpec((tq, GROUP_W), lambda b, i: (b * nq + i, 0)),
        out_shape=jax.ShapeDtypeStruct((n, GROUP_W), jnp.bfloat16),
        scratch_shapes=[pltpu.VMEM((seq, GROUP_W), jnp.bfloat16), pltpu.VMEM((seq, GROUP_W), jnp.bfloat16)],
        compiler_params=_params(2),
        name="fourier",
    )(proj, c128, ms128, cs, mss)


def _diff_attn_kernel(lam_ref, subln_ref, q_ref, k_ref, v_ref, cq_ref, uq_ref, dq_ref,
                      ck_ref, uk_ref, dk_ref, o_ref, kr_ref, *, lam_init):
    @pl.when(pl.program_id(2) == 0)
    def _():
        k = k_ref[...].astype(jnp.float32)
        kr_ref[...] = _rope(k, ck_ref[...], uk_ref[...], dk_ref[...]).astype(kr_ref.dtype)

    tq = q_ref.shape[0]
    q = _rope(q_ref[...].astype(jnp.float32), cq_ref[...], uq_ref[...], dq_ref[...]) * (DIFF_DIM ** -0.5)
    lane = lax.broadcasted_iota(jnp.int32, (1, HEAD_DIM), 1)
    q0 = jnp.where(lane < DIFF_DIM, q, 0.0).astype(jnp.bfloat16)
    q1 = jnp.where(lane >= DIFF_DIM, q, 0.0).astype(jnp.bfloat16)
    s = lax.dot_general(jnp.concatenate([q0, q1], axis=0), kr_ref[...], _NT,
                        preferred_element_type=jnp.float32)
    e = jnp.exp(s - jnp.max(s, axis=-1, keepdims=True))
    r = 1.0 / jnp.sum(e, axis=-1, keepdims=True)
    lp = lam_ref[...]
    lam = (jnp.exp(jnp.sum(lp[0:1] * lp[1:2], axis=-1, keepdims=True))
           - jnp.exp(jnp.sum(lp[2:3] * lp[3:4], axis=-1, keepdims=True)) + lam_init)
    p = e[:tq] * r[:tq] - lam * (e[tq:] * r[tq:])
    o = jnp.dot(p.astype(jnp.bfloat16), v_ref[...], preferred_element_type=jnp.float32)
    o_ref[...] = (_rms(o, subln_ref[...]) * (1.0 - lam_init)).astype(o_ref.dtype)


def _diff_attn(proj, lam_params, subln, layer_idx, batch, seq, tq=128):
    n = batch * seq
    nq = seq // tq
    ang = _rope_angles(jnp.arange(seq), DIFF_DIM)
    cos, up, dn = _rope_tables(ang, ang)
    lam_init = 0.8 - 0.6 * math.exp(-0.3 * layer_idx)
    qtab = pl.BlockSpec((tq, HEAD_DIM), lambda b, h, i: (i, 0))
    ktab = pl.BlockSpec((seq, HEAD_DIM), lambda b, h, i: (0, 0))
    return pl.pallas_call(
        functools.partial(_diff_attn_kernel, lam_init=lam_init),
        grid=(batch, DIFF_HEADS, nq),
        in_specs=[pl.BlockSpec((4, DIFF_DIM), lambda b, h, i: (0, 0)),
                  pl.BlockSpec((1, HEAD_DIM), lambda b, h, i: (0, 0)),
                  pl.BlockSpec((tq, HEAD_DIM), lambda b, h, i: (b * nq + i, OFF_BQ // HEAD_DIM + h)),
                  pl.BlockSpec((seq, HEAD_DIM), lambda b, h, i: (b, OFF_BK // HEAD_DIM + h)),
                  pl.BlockSpec((seq, HEAD_DIM), lambda b, h, i: (b, OFF_BV // HEAD_DIM + h)),
                  qtab, qtab, qtab, ktab, ktab, ktab],
        out_specs=pl.BlockSpec((tq, HEAD_DIM), lambda b, h, i: (b * nq + i, h)),
        out_shape=jax.ShapeDtypeStruct((n, GROUP_W), jnp.bfloat16),
        scratch_shapes=[pltpu.VMEM((seq, HEAD_DIM), jnp.bfloat16)],
        compiler_params=_params(3),
        name="diff_attn",
    )(lam_params, subln.reshape(1, HEAD_DIM), proj, proj, proj, cos, up, dn, cos, up, dn)


def _gqa_kernel(qn_ref, kn_ref, q_ref, k_ref, v_ref, cq_ref, uq_ref, dq_ref,
                ck_ref, uk_ref, dk_ref, o_ref, kr_ref):
    @pl.when(pl.program_id(2) == 0)
    def _():
        k = _rms(k_ref[...].astype(jnp.float32), kn_ref[...])
        kr_ref[...] = _rope(k, ck_ref[...], uk_ref[...], dk_ref[...]).astype(kr_ref.dtype)

    tq = q_ref.shape[0]
    cq, uq, dq = cq_ref[...], uq_ref[...], dq_ref[...]
    heads = []
    for h in range(GQA_GROUP):
        qh = _rms(q_ref[:, h * HEAD_DIM:(h + 1) * HEAD_DIM].astype(jnp.float32), qn_ref[...])
        heads.append((_rope(qh, cq, uq, dq) * (HEAD_DIM ** -0.5)).astype(jnp.bfloat16))
    s = lax.dot_general(jnp.concatenate(heads, axis=0), kr_ref[...], _NT,
                        preferred_element_type=jnp.float32)
    e = jnp.exp(s - jnp.max(s, axis=-1, keepdims=True))
    r = 1.0 / jnp.sum(e, axis=-1, keepdims=True)
    o = jnp.dot(e.astype(jnp.bfloat16), v_ref[...], preferred_element_type=jnp.float32) * r
    for h in range(GQA_GROUP):
        o_ref[:, h * HEAD_DIM:(h + 1) * HEAD_DIM] = o[h * tq:(h + 1) * tq].astype(o_ref.dtype)


def _gqa(proj, q_norm, k_norm, batch, seq, tq=64):
    n = batch * seq
    nq = seq // tq
    t = jnp.arange(seq)
    cos, up, dn = _rope_tables(_rope_angles(t // GRID_W, HEAD_DIM // 2), _rope_angles(t % GRID_W, HEAD_DIM // 2))
    qw = GQA_GROUP * HEAD_DIM
    qtab = pl.BlockSpec((tq, HEAD_DIM), lambda b, g, i: (i, 0))
    ktab = pl.BlockSpec((seq, HEAD_DIM), lambda b, g, i: (0, 0))
    gain = pl.BlockSpec((1, HEAD_DIM), lambda b, g, i: (0, 0))
    return pl.pallas_call(
        _gqa_kernel,
        grid=(batch, GQA_KV_HEADS, nq),
        in_specs=[gain, gain,
                  pl.BlockSpec((tq, qw), lambda b, g, i: (b * nq + i, OFF_CQ // qw + g)),
                  pl.BlockSpec((seq, HEAD_DIM), lambda b, g, i: (b, OFF_CK // HEAD_DIM + g)),
                  pl.BlockSpec((seq, HEAD_DIM), lambda b, g, i: (b, OFF_CV // HEAD_DIM + g)),
                  qtab, qtab, qtab, ktab, ktab, ktab],
        out_specs=pl.BlockSpec((tq, qw), lambda b, g, i: (b * nq + i, g)),
        out_shape=jax.ShapeDtypeStruct((n, GROUP_W), jnp.bfloat16),
        scratch_shapes=[pltpu.VMEM((seq, HEAD_DIM), jnp.bfloat16)],
        compiler_params=_params(3),
        name="gqa",
    )(q_norm.reshape(1, HEAD_DIM), k_norm.reshape(1, HEAD_DIM), proj, proj, proj, cos, up, dn, cos, up, dn)


def _na_bias_table(rel_bias):
    cls = jnp.arange(NA_KH)
    j = jnp.arange(NA_KH)
    row_off = j[None, :] - cls[:, None] + (NA_KH - 1)
    c = jnp.arange(GRID_W)
    col_start = jnp.clip(c - NA_KW // 2, 0, GRID_W - NA_KW)
    col_mask = (c[None, :] >= col_start[:, None]) & (c[None, :] < col_start[:, None] + NA_KW)
    col_off = jnp.clip(c[None, :] - c[:, None] + (NA_KW - 1), 0, 2 * NA_KW - 2)
    bias = rel_bias[:, row_off[:, None, :, None], col_off[None, :, None, :]]
    bias = jnp.where(col_mask[None, None, :, None, :], bias.astype(jnp.float32), MASK_VALUE)
    return bias.reshape(NA_HEADS, NA_KH, GRID_W, NA_KH * GRID_W)


def _na_kernel(q_ref, k_ref, v_ref, bias_ref, o_ref, *, rows):
    win = NA_KH * GRID_W

    def body(r, carry):
        start = jnp.clip(r - NA_KH // 2, 0, rows - NA_KH)
        q0 = pl.multiple_of(r * GRID_W, GRID_W)
        k0 = pl.multiple_of(start * GRID_W, GRID_W)
        s = lax.dot_general(q_ref[pl.ds(q0, GRID_W), :], k_ref[pl.ds(k0, win), :], _NT,
                            preferred_element_type=jnp.float32)
        s = s * (HEAD_DIM ** -0.5) + bias_ref[0, r - start]
        e = jnp.exp(s - jnp.max(s, axis=-1, keepdims=True))
        r_sum = 1.0 / jnp.sum(e, axis=-1, keepdims=True)
        o = jnp.dot(e.astype(jnp.bfloat16), v_ref[pl.ds(k0, win), :], preferred_element_type=jnp.float32)
        o_ref[pl.ds(q0, GRID_W), :] = (o * r_sum).astype(o_ref.dtype)
        return carry

    lax.fori_loop(0, rows, body, 0)


def _na(proj, rel_bias, batch, seq):
    n = batch * seq
    rows = seq // GRID_W
    assert rows >= NA_KH
    bias = _na_bias_table(rel_bias)
    blk = lambda off: pl.BlockSpec((seq, HEAD_DIM), lambda b, h: (b, off // HEAD_DIM + h))
    return pl.pallas_call(
        functools.partial(_na_kernel, rows=rows),
        grid=(batch, NA_HEADS),
        in_specs=[blk(OFF_DQ), blk(OFF_DK), blk(OFF_DV),
                  pl.BlockSpec((1, NA_KH, GRID_W, NA_KH * GRID_W), lambda b, h: (h, 0, 0, 0))],
        out_specs=pl.BlockSpec((seq, HEAD_DIM), lambda b, h: (b, h)),
        out_shape=jax.ShapeDtypeStruct((n, GROUP_W), jnp.bfloat16),
        compiler_params=_params(2),
        name="nbr_attn",
    )(proj, proj, proj, bias)


def _norm_router_kernel(x_ref, g_ref, wr_ref, h_ref, aff_ref):
    h = _rms(x_ref[...], g_ref[...]).astype(jnp.bfloat16)
    h_ref[...] = h
    logits = jnp.dot(h, wr_ref[...], preferred_element_type=jnp.float32)
    lane = lax.broadcasted_iota(jnp.int32, logits.shape, 1)
    logits = jnp.where(lane < N_EXPERTS, logits, MASK_VALUE)
    e = jnp.exp(logits - jnp.max(logits, axis=-1, keepdims=True))
    aff_ref[...] = e / jnp.sum(e, axis=-1, keepdims=True)


def _norm_router(x, gain, w_router_padded, tm=256):
    n, d = x.shape
    return pl.pallas_call(
        _norm_router_kernel,
        grid=(n // tm,),
        in_specs=[pl.BlockSpec((tm, d), lambda i: (i, 0)),
                  pl.BlockSpec((1, d), lambda i: (0, 0)),
                  pl.BlockSpec((d, V7X_LANES), lambda i: (0, 0))],
        out_specs=[pl.BlockSpec((tm, d), lambda i: (i, 0)), pl.BlockSpec((tm, V7X_LANES), lambda i: (i, 0))],
        out_shape=[jax.ShapeDtypeStruct((n, d), jnp.bfloat16), jax.ShapeDtypeStruct((n, V7X_LANES), jnp.float32)],
        compiler_params=_params(1),
        name="ffn_norm_router",
    )(x, gain.reshape(1, d), w_router_padded)


def _expert_kernel(xe_ref, gate_ref, wg_ref, wu_ref, wd_ref, o_ref):
    f = pl.program_id(2)
    xe = xe_ref[0]
    g = jnp.dot(xe, wg_ref[0].astype(jnp.bfloat16), preferred_element_type=jnp.float32)
    u = jnp.dot(xe, wu_ref[0].astype(jnp.bfloat16), preferred_element_type=jnp.float32)
    hidden = ((g / (1.0 + jnp.exp(-g))) * u).astype(jnp.bfloat16)
    part = jnp.dot(hidden, wd_ref[0].astype(jnp.bfloat16), preferred_element_type=jnp.float32)

    @pl.when(f == 0)
    def _():
        o_ref[0] = part

    @pl.when(f > 0)
    def _():
        o_ref[0] += part

    @pl.when(f == pl.num_programs(2) - 1)
    def _():
        o_ref[0] = o_ref[0] * gate_ref[0]


def _experts(xe, gate, w_gate, w_up, w_down, tm=512, tf=128):
    n_exp, t, d = xe.shape
    d_ff = w_gate.shape[-1]
    tm = min(tm, t)
    return pl.pallas_call(
        _expert_kernel,
        grid=(n_exp, t // tm, d_ff // tf),
        in_specs=[pl.BlockSpec((1, tm, d), lambda e, i, f: (e, i, 0)),
                  pl.BlockSpec((1, tm, 1), lambda e, i, f: (e, i, 0)),
                  pl.BlockSpec((1, d, tf), lambda e, i, f: (e, 0, f)),
                  pl.BlockSpec((1, d, tf), lambda e, i, f: (e, 0, f)),
                  pl.BlockSpec((1, tf, d), lambda e, i, f: (e, f, 0))],
        out_specs=pl.BlockSpec((1, tm, d), lambda e, i, f: (e, i, 0)),
        out_shape=jax.ShapeDtypeStruct((n_exp, t, d), jnp.float32),
        compiler_params=_params(3),
        name="experts",
    )(xe, gate, w_gate, w_up, w_down)


def _mixer_layer(x, batch, seq, layer, attn_norm, w_in, diff_lambda, diff_subln, gqa_q_norm, gqa_k_norm,
                 na_rel_bias, w_out):
    h = _rmsnorm(x, attn_norm, jnp.bfloat16)
    proj = _matmul(h, w_in, jnp.bfloat16)
    mixed = (_fourier(proj, batch, seq),
             _diff_attn(proj, diff_lambda, diff_subln, layer, batch, seq),
             _gqa(proj, gqa_q_norm, gqa_k_norm, batch, seq),
             _na(proj, na_rel_bias, batch, seq))
    return _out_proj(mixed, w_out, x)


def _ffn_layer(xs, ffn_norm, w_router, w_gate, w_up, w_down):
    wr = jnp.pad(w_router, ((0, 0), (0, V7X_LANES - N_EXPERTS))).astype(jnp.bfloat16)
    xes, gates, toks = [], [], []
    for x in xs:
        n = x.shape[0]
        cap = EC_CAPACITY * n // N_EXPERTS
        h, aff = _norm_router(x, ffn_norm, wr)
        gate, tok = lax.top_k(aff[:, :N_EXPERTS].T, cap)
        xes.append(h[tok])
        gates.append(gate)
        toks.append(tok)
    ye = _experts(jnp.concatenate(xes, axis=1), jnp.concatenate(gates, axis=1)[..., None],
                  w_gate, w_up, w_down)
    out, start = [], 0
    for x, tok in zip(xs, toks):
        cap = tok.shape[1]
        y = ye[:, start:start + cap].reshape(-1, x.shape[1])
        out.append(x.at[tok.reshape(-1)].add(y))
        start += cap
    return out


def kernel(x_prompt, x_sample, attn_norm, w_in, diff_lambda, diff_subln, gqa_q_norm, gqa_k_norm, na_rel_bias,
           w_out, ffn_norm, w_router, w_gate, w_up, w_down, final_norm):
    shapes = [x_prompt.shape, x_sample.shape]
    xs = [x_prompt.reshape(-1, D_MODEL), x_sample.reshape(-1, D_MODEL)]
    w_in_b = w_in.astype(jnp.bfloat16)
    w_out_b = w_out.astype(jnp.bfloat16)
    for l in range(w_in.shape[0]):
        xs = [_mixer_layer(x, shp[0], shp[1], l, attn_norm[l], w_in_b[l], diff_lambda[l], diff_subln[l],
                           gqa_q_norm[l], gqa_k_norm[l], na_rel_bias[l], w_out_b[l])
              for x, shp in zip(xs, shapes)]
        xs = _ffn_layer(xs, ffn_norm[l], w_router[l], w_gate[l], w_up[l], w_down[l])
    return tuple(_rmsnorm(x, final_norm, jnp.float32).reshape(shp) for x, shp in zip(xs, shapes))
```

```python
import functools
import math

import jax
import jax.numpy as jnp
from jax import lax
from jax.experimental import pallas as pl
from jax.experimental.pallas import tpu as pltpu

D_MODEL = 4096
DEPTH = 2
GRID_W = 64
HEAD_DIM = 128
RMS_EPS = 1e-6
ROPE_THETA = 10000.0
GROUP_W = D_MODEL // 4
FNET_GROUPS = 8
FNET_GROUP_W = GROUP_W // FNET_GROUPS
DIFF_HEADS = GROUP_W // HEAD_DIM
DIFF_DIM = HEAD_DIM // 2
GQA_HEADS = GROUP_W // HEAD_DIM
GQA_KV_HEADS = 2
GQA_GROUP = GQA_HEADS // GQA_KV_HEADS
NA_HEADS = GROUP_W // HEAD_DIM
NA_KH = 8
NA_KW = 16
N_EXPERTS = 16
EC_CAPACITY = 2
D_FF_EXPERT = D_MODEL

OFF_A = 0
OFF_BQ = OFF_A + GROUP_W
OFF_BK = OFF_BQ + GROUP_W
OFF_BV = OFF_BK + GROUP_W
OFF_CQ = OFF_BV + GROUP_W
OFF_CK = OFF_CQ + GROUP_W
OFF_CV = OFF_CK + GQA_KV_HEADS * HEAD_DIM
OFF_DQ = OFF_CV + GQA_KV_HEADS * HEAD_DIM
OFF_DK = OFF_DQ + GROUP_W
OFF_DV = OFF_DK + GROUP_W
IN_WIDTH = OFF_DV + GROUP_W

V7X_LANES = 128
V7X_VMEM_BYTES = 64 * 1024 * 1024
VMEM_LIMIT = V7X_VMEM_BYTES - 8 * 1024 * 1024

MASK_VALUE = -1e30

_NT = (((1,), (1,)), ((), ()))


def _params(n_grid_dims):
    return pltpu.CompilerParams(
        dimension_semantics=("arbitrary",) * n_grid_dims, vmem_limit_bytes=VMEM_LIMIT)


def _rms(x, gain):
    ms = jnp.mean(x * x, axis=-1, keepdims=True)
    return (x * lax.rsqrt(ms + RMS_EPS)) * gain


def _rope(x, cos, sin_up, sin_dn):
    return x * cos + pltpu.roll(x, 96, 1) * sin_up + pltpu.roll(x, 32, 1) * sin_dn


def _rope_tables(ang_lo, ang_hi):
    z = jnp.zeros_like(ang_lo)
    c_lo, s_lo, c_hi, s_hi = jnp.cos(ang_lo), jnp.sin(ang_lo), jnp.cos(ang_hi), jnp.sin(ang_hi)
    cos = jnp.concatenate([c_lo, c_lo, c_hi, c_hi], axis=-1)
    sin_up = jnp.concatenate([-s_lo, z, -s_hi, z], axis=-1)
    sin_dn = jnp.concatenate([z, s_lo, z, s_hi], axis=-1)
    return cos, sin_up, sin_dn


def _rope_angles(pos, dim):
    inv = ROPE_THETA ** (-jnp.arange(0, dim, 2, dtype=jnp.float32) / dim)
    return pos.astype(jnp.float32)[:, None] * inv[None, :]


def _rmsnorm_kernel(x_ref, g_ref, o_ref):
    o_ref[...] = _rms(x_ref[...], g_ref[...]).astype(o_ref.dtype)


def _rmsnorm(x, gain, out_dtype, tm=256):
    n, d = x.shape
    return pl.pallas_call(
        _rmsnorm_kernel,
        grid=(n // tm,),
        in_specs=[pl.BlockSpec((tm, d), lambda i: (i, 0)), pl.BlockSpec((1, d), lambda i: (0, 0))],
        out_specs=pl.BlockSpec((tm, d), lambda i: (i, 0)),
        out_shape=jax.ShapeDtypeStruct((n, d), out_dtype),
        compiler_params=_params(1),
        name="rmsnorm",
    )(x, gain.reshape(1, d))


def _matmul_kernel(a_ref, b_ref, o_ref):
    o_ref[...] = jnp.dot(a_ref[...], b_ref[...], preferred_element_type=jnp.float32).astype(o_ref.dtype)


def _matmul(a, b, out_dtype, tm=1024, tn=512):
    m, k = a.shape
    _, n = b.shape
    tm = min(tm, m)
    return pl.pallas_call(
        _matmul_kernel,
        grid=(m // tm, n // tn),
        in_specs=[pl.BlockSpec((tm, k), lambda i, j: (i, 0)), pl.BlockSpec((k, tn), lambda i, j: (0, j))],
        out_specs=pl.BlockSpec((tm, tn), lambda i, j: (i, j)),
        out_shape=jax.ShapeDtypeStruct((m, n), out_dtype),
        compiler_params=_params(2),
        name="in_proj",
    )(a, b)


def _out_proj_kernel(fa_ref, db_ref, gc_ref, nd_ref, w_ref, x_ref, o_ref):
    acc = x_ref[...]
    for i, a_ref in enumerate((fa_ref, db_ref, gc_ref, nd_ref)):
        acc = acc + jnp.dot(a_ref[...], w_ref[i * GROUP_W:(i + 1) * GROUP_W, :],
                            preferred_element_type=jnp.float32)
    o_ref[...] = acc


def _out_proj(mixed, w, x, tm=1024, tn=512):
    n, d = x.shape
    tm = min(tm, n)
    a_spec = pl.BlockSpec((tm, GROUP_W), lambda i, j: (i, 0))
    return pl.pallas_call(
        _out_proj_kernel,
        grid=(n // tm, d // tn),
        in_specs=[a_spec, a_spec, a_spec, a_spec,
                  pl.BlockSpec((d, tn), lambda i, j: (0, j)),
                  pl.BlockSpec((tm, tn), lambda i, j: (i, j))],
        out_specs=pl.BlockSpec((tm, tn), lambda i, j: (i, j)),
        out_shape=jax.ShapeDtypeStruct((n, d), jnp.float32),
        compiler_params=_params(2),
        name="out_proj",
    )(*mixed, w, x)


def _dft_tables(n):
    idx = jnp.arange(n, dtype=jnp.int32)
    ang = ((idx[:, None] * idx[None, :]) % n).astype(jnp.float32) * (2.0 * math.pi / n)
    return jnp.cos(ang).astype(jnp.bfloat16), (-jnp.sin(ang)).astype(jnp.bfloat16)


def _fourier_kernel(a_ref, c128_ref, ms128_ref, cs_ref, mss_ref, o_ref, ac_ref, as_ref, *, scale):
    @pl.when(pl.program_id(1) == 0)
    def _():
        for g in range(FNET_GROUPS):
            cols = slice(g * FNET_GROUP_W, (g + 1) * FNET_GROUP_W)
            a = a_ref[:, cols]
            ac_ref[:, cols] = jnp.dot(a, c128_ref[...], preferred_element_type=jnp.float32).astype(ac_ref.dtype)
            as_ref[:, cols] = jnp.dot(a, ms128_ref[...], preferred_element_type=jnp.float32).astype(as_ref.dtype)

    o = jnp.dot(cs_ref[...], ac_ref[...], preferred_element_type=jnp.float32)
    o = o - jnp.dot(mss_ref[...], as_ref[...], preferred_element_type=jnp.float32)
    o_ref[...] = (o * scale).astype(o_ref.dtype)


def _fourier(proj, batch, seq, tq=256):
    n = batch * seq
    tq = min(tq, seq)
    cs, mss = _dft_tables(seq)
    c128, ms128 = _dft_tables(FNET_GROUP_W)
    nq = seq // tq
    kern = functools.partial(_fourier_kernel, scale=1.0 / math.sqrt(seq * FNET_GROUP_W))
    return pl.pallas_call(
        kern,
        grid=(batch, nq),
        in_specs=[pl.BlockSpec((seq, GROUP_W), lambda b, i: (b, OFF_A // GROUP_W)),
                  pl.BlockSpec((FNET_GROUP_W, FNET_GROUP_W), lambda b, i: (0, 0)),
                  pl.BlockSpec((FNET_GROUP_W, FNET_GROUP_W), lambda b, i: (0, 0)),
                  pl.BlockSpec((tq, seq), lambda b, i: (i, 0)),
                  pl.BlockSpec((tq, seq), lambda b, i: (i, 0))],
        out_specs=pl.BlockSpec((tq, GROUP_W), lambda b, i: (b * nq + i, 0)),
        out_shape=jax.ShapeDtypeStruct((n, GROUP_W), jnp.bfloat16),
        scratch_shapes=[pltpu.VMEM((seq, GROUP_W), jnp.bfloat16), pltpu.VMEM((seq, GROUP_W), jnp.bfloat16)],
        compiler_params=_params(2),
        name="fourier",
    )(proj, c128, ms128, cs, mss)


def _diff_attn_kernel(lam_ref, subln_ref, q_ref, k_ref, v_ref, cq_ref, uq_ref, dq_ref,
                      ck_ref, uk_ref, dk_ref, o_ref, kr_ref, *, lam_init):
    @pl.when(pl.program_id(2) == 0)
    def _():
        k = k_ref[...].astype(jnp.float32)
        kr_ref[...] = _rope(k, ck_ref[...], uk_ref[...], dk_ref[...]).astype(kr_ref.dtype)

    tq = q_ref.shape[0]
    q = _rope(q_ref[...].astype(jnp.float32), cq_ref[...], uq_ref[...], dq_ref[...]) * (DIFF_DIM ** -0.5)
    lane = lax.broadcasted_iota(jnp.int32, (1, HEAD_DIM), 1)
    q0 = jnp.where(lane < DIFF_DIM, q, 0.0).astype(jnp.bfloat16)
    q1 = jnp.where(lane >= DIFF_DIM, q, 0.0).astype(jnp.bfloat16)
    s = lax.dot_general(jnp.concatenate([q0, q1], axis=0), kr_ref[...], _NT,
                        preferred_element_type=jnp.float32)
    e = jnp.exp(s - jnp.max(s, axis=-1, keepdims=True))
    r = 1.0 / jnp.sum(e, axis=-1, keepdims=True)
    lp = lam_ref[...]
    lam = (jnp.exp(jnp.sum(lp[0:1] * lp[1:2], axis=-1, keepdims=True))
           - jnp.exp(jnp.sum(lp[2:3] * lp[3:4], axis=-1, keepdims=True)) + lam_init)
    p = e[:tq] * r[:tq] - lam * (e[tq:] * r[tq:])
    o = jnp.dot(p.astype(jnp.bfloat16), v_ref[...], preferred_element_type=jnp.float32)
    o_ref[...] = (_rms(o, subln_ref[...]) * (1.0 - lam_init)).astype(o_ref.dtype)


SCORE_TILE_ELEMS = 1 << 20


def _diff_attn(proj, lam_params, subln, layer_idx, batch, seq):
    n = batch * seq
    tq = min(seq, SCORE_TILE_ELEMS // (2 * seq))
    nq = seq // tq
    ang = _rope_angles(jnp.arange(seq), DIFF_DIM)
    cos, up, dn = _rope_tables(ang, ang)
    lam_init = 0.8 - 0.6 * math.exp(-0.3 * layer_idx)
    qtab = pl.BlockSpec((tq, HEAD_DIM), lambda b, h, i: (i, 0))
    ktab = pl.BlockSpec((seq, HEAD_DIM), lambda b, h, i: (0, 0))
    return pl.pallas_call(
        functools.partial(_diff_attn_kernel, lam_init=lam_init),
        grid=(batch, DIFF_HEADS, nq),
        in_specs=[pl.BlockSpec((4, DIFF_DIM), lambda b, h, i: (0, 0)),
                  pl.BlockSpec((1, HEAD_DIM), lambda b, h, i: (0, 0)),
                  pl.BlockSpec((tq, HEAD_DIM), lambda b, h, i: (b * nq + i, OFF_BQ // HEAD_DIM + h)),
                  pl.BlockSpec((seq, HEAD_DIM), lambda b, h, i: (b, OFF_BK // HEAD_DIM + h)),
                  pl.BlockSpec((seq, HEAD_DIM), lambda b, h, i: (b, OFF_BV // HEAD_DIM + h)),
                  qtab, qtab, qtab, ktab, ktab, ktab],
        out_specs=pl.BlockSpec((tq, HEAD_DIM), lambda b, h, i: (b * nq + i, h)),
        out_shape=jax.ShapeDtypeStruct((n, GROUP_W), jnp.bfloat16),
        scratch_shapes=[pltpu.VMEM((seq, HEAD_DIM), jnp.bfloat16)],
        compiler_params=_params(3),
        name="diff_attn",
    )(lam_params, subln.reshape(1, HEAD_DIM), proj, proj, proj, cos, up, dn, cos, up, dn)


def _gqa_kernel(qn_ref, kn_ref, q_ref, k_ref, v_ref, cq_ref, uq_ref, dq_ref,
                ck_ref, uk_ref, dk_ref, o_ref, kr_ref):
    @pl.when(pl.program_id(2) == 0)
    def _():
        k = _rms(k_ref[...].astype(jnp.float32), kn_ref[...])
        kr_ref[...] = _rope(k, ck_ref[...], uk_ref[...], dk_ref[...]).astype(kr_ref.dtype)

    tq = q_ref.shape[0]
    cq, uq, dq = cq_ref[...], uq_ref[...], dq_ref[...]
    heads = []
    for h in range(GQA_GROUP):
        qh = _rms(q_ref[:, h * HEAD_DIM:(h + 1) * HEAD_DIM].astype(jnp.float32), qn_ref[...])
        heads.append((_rope(qh, cq, uq, dq) * (HEAD_DIM ** -0.5)).astype(jnp.bfloat16))
    s = lax.dot_general(jnp.concatenate(heads, axis=0), kr_ref[...], _NT,
                        preferred_element_type=jnp.float32)
    e = jnp.exp(s - jnp.max(s, axis=-1, keepdims=True))
    r = 1.0 / jnp.sum(e, axis=-1, keepdims=True)
    o = jnp.dot(e.astype(jnp.bfloat16), v_ref[...], preferred_element_type=jnp.float32) * r
    for h in range(GQA_GROUP):
        o_ref[:, h * HEAD_DIM:(h + 1) * HEAD_DIM] = o[h * tq:(h + 1) * tq].astype(o_ref.dtype)


def _gqa(proj, q_norm, k_norm, batch, seq):
    n = batch * seq
    tq = min(seq, SCORE_TILE_ELEMS // (GQA_GROUP * seq))
    nq = seq // tq
    t = jnp.arange(seq)
    cos, up, dn = _rope_tables(_rope_angles(t // GRID_W, HEAD_DIM // 2), _rope_angles(t % GRID_W, HEAD_DIM // 2))
    qw = GQA_GROUP * HEAD_DIM
    qtab = pl.BlockSpec((tq, HEAD_DIM), lambda b, g, i: (i, 0))
    ktab = pl.BlockSpec((seq, HEAD_DIM), lambda b, g, i: (0, 0))
    gain = pl.BlockSpec((1, HEAD_DIM), lambda b, g, i: (0, 0))
    return pl.pallas_call(
        _gqa_kernel,
        grid=(batch, GQA_KV_HEADS, nq),
        in_specs=[gain, gain,
                  pl.BlockSpec((tq, qw), lambda b, g, i: (b * nq + i, OFF_CQ // qw + g)),
                  pl.BlockSpec((seq, HEAD_DIM), lambda b, g, i: (b, OFF_CK // HEAD_DIM + g)),
                  pl.BlockSpec((seq, HEAD_DIM), lambda b, g, i: (b, OFF_CV // HEAD_DIM + g)),
                  qtab, qtab, qtab, ktab, ktab, ktab],
        out_specs=pl.BlockSpec((tq, qw), lambda b, g, i: (b * nq + i, g)),
        out_shape=jax.ShapeDtypeStruct((n, GROUP_W), jnp.bfloat16),
        scratch_shapes=[pltpu.VMEM((seq, HEAD_DIM), jnp.bfloat16)],
        compiler_params=_params(3),
        name="gqa",
    )(q_norm.reshape(1, HEAD_DIM), k_norm.reshape(1, HEAD_DIM), proj, proj, proj, cos, up, dn, cos, up, dn)


def _na_bias_table(rel_bias):
    cls = jnp.arange(NA_KH)
    j = jnp.arange(NA_KH)
    row_off = j[None, :] - cls[:, None] + (NA_KH - 1)
    c = jnp.arange(GRID_W)
    col_start = jnp.clip(c - NA_KW // 2, 0, GRID_W - NA_KW)
    col_mask = (c[None, :] >= col_start[:, None]) & (c[None, :] < col_start[:, None] + NA_KW)
    col_off = jnp.clip(c[None, :] - c[:, None] + (NA_KW - 1), 0, 2 * NA_KW - 2)
    bias = rel_bias[:, row_off[:, None, :, None], col_off[None, :, None, :]]
    bias = jnp.where(col_mask[None, None, :, None, :], bias.astype(jnp.float32), MASK_VALUE)
    return bias.reshape(NA_HEADS, NA_KH, GRID_W, NA_KH * GRID_W)


def _na_kernel(q_ref, k_ref, v_ref, bias_ref, o_ref, *, rows):
    win = NA_KH * GRID_W

    def body(r, carry):
        start = jnp.clip(r - NA_KH // 2, 0, rows - NA_KH)
        q0 = pl.multiple_of(r * GRID_W, GRID_W)
        k0 = pl.multiple_of(start * GRID_W, GRID_W)
        s = lax.dot_general(q_ref[pl.ds(q0, GRID_W), :], k_ref[pl.ds(k0, win), :], _NT,
                            preferred_element_type=jnp.float32)
        s = s * (HEAD_DIM ** -0.5) + bias_ref[0, r - start]
        e = jnp.exp(s - jnp.max(s, axis=-1, keepdims=True))
        r_sum = 1.0 / jnp.sum(e, axis=-1, keepdims=True)
        o = jnp.dot(e.astype(jnp.bfloat16), v_ref[pl.ds(k0, win), :], preferred_element_type=jnp.float32)
        o_ref[pl.ds(q0, GRID_W), :] = (o * r_sum).astype(o_ref.dtype)
        return carry

    lax.fori_loop(0, rows, body, 0, unroll=NA_KH)


def _na(proj, rel_bias, batch, seq):
    n = batch * seq
    rows = seq // GRID_W
    assert rows >= NA_KH
    bias = _na_bias_table(rel_bias)
    blk = lambda off: pl.BlockSpec((seq, HEAD_DIM), lambda b, h: (b, off // HEAD_DIM + h))
    return pl.pallas_call(
        functools.partial(_na_kernel, rows=rows),
        grid=(batch, NA_HEADS),
        in_specs=[blk(OFF_DQ), blk(OFF_DK), blk(OFF_DV),
                  pl.BlockSpec((1, NA_KH, GRID_W, NA_KH * GRID_W), lambda b, h: (h, 0, 0, 0))],
        out_specs=pl.BlockSpec((seq, HEAD_DIM), lambda b, h: (b, h)),
        out_shape=jax.ShapeDtypeStruct((n, GROUP_W), jnp.bfloat16),
        compiler_params=_params(2),
        name="nbr_attn",
    )(proj, proj, proj, bias)


def _norm_router_kernel(x_ref, g_ref, wr_ref, hp_ref, aff_ref):
    h = _rms(x_ref[...], g_ref[...]).astype(jnp.bfloat16)
    half = h.shape[1] // 2
    h32 = h.astype(jnp.float32)
    lo = pltpu.bitcast(h32[:, :half], jnp.uint32) >> 16
    hi = pltpu.bitcast(h32[:, half:], jnp.uint32) & jnp.uint32(0xFFFF0000)
    hp_ref[...] = lo | hi
    logits = jnp.dot(h, wr_ref[...], preferred_element_type=jnp.float32)
    lane = lax.broadcasted_iota(jnp.int32, logits.shape, 1)
    logits = jnp.where(lane < N_EXPERTS, logits, MASK_VALUE)
    e = jnp.exp(logits - jnp.max(logits, axis=-1, keepdims=True))
    aff_ref[...] = e / jnp.sum(e, axis=-1, keepdims=True)


def _norm_router(x, gain, w_router_padded, tm=256):
    n, d = x.shape
    return pl.pallas_call(
        _norm_router_kernel,
        grid=(n // tm,),
        in_specs=[pl.BlockSpec((tm, d), lambda i: (i, 0)),
                  pl.BlockSpec((1, d), lambda i: (0, 0)),
                  pl.BlockSpec((d, V7X_LANES), lambda i: (0, 0))],
        out_specs=[pl.BlockSpec((tm, d // 2), lambda i: (i, 0)), pl.BlockSpec((tm, V7X_LANES), lambda i: (i, 0))],
        out_shape=[jax.ShapeDtypeStruct((n, d // 2), jnp.uint32), jax.ShapeDtypeStruct((n, V7X_LANES), jnp.float32)],
        compiler_params=_params(1),
        name="ffn_norm_router",
    )(x, gain.reshape(1, d), w_router_padded)


N_GROUPS = 2
_SEM_XE, _SEM_ROWS, _SEM_OUT = 0, 1, 2


def _expert_kernel(tok_ref, tok_next_ref, gate_ref, wg_ref, wu_ref, wd_ref, hp0_ref, hp1_ref, xin0_ref, xin1_ref,
                   x0_ref, x1_ref, xe_buf, hid_buf, row_buf, sems, *, nf, nd, tm, tf, tn, tiles, tiles_per_group):
    del xin0_ref, xin1_ref
    e, i, j = pl.program_id(0), pl.program_id(1), pl.program_id(2)
    first = jnp.logical_and(e == 0, i == 0)
    last = jnp.logical_and(e == pl.num_programs(0) - 1, i == tiles - 1)
    hp_refs, x_refs = (hp0_ref, hp1_ref), (x0_ref, x1_ref)
    group = i // tiles_per_group
    group_next = ((i + 1) % tiles) // tiles_per_group

    def per_group(which, fn):
        for g in range(N_GROUPS):
            pl.when(which == g)(functools.partial(fn, g))

    def row_copy(src, src_row, dst, dst_row, sem):
        return pltpu.make_async_copy(src.at[pl.ds(src_row, 1)], dst.at[pl.ds(dst_row, 1)], sems.at[sem])

    def gather(idx_ref, src, dst, sem):
        def body(r, carry):
            row_copy(src, idx_ref[0, 0, r], dst, r, sem).start()
            return carry
        lax.fori_loop(0, tm, body, 0, unroll=8)

    def scatter(idx_ref, src, dst, sem):
        def body(r, carry):
            row_copy(src, r, dst, idx_ref[0, 0, r], sem).start()
            return carry
        lax.fori_loop(0, tm, body, 0, unroll=8)

    def wait_rows(src, dst, sem):
        def body(r, carry):
            row_copy(src, 0, dst, 0, sem).wait()
            return carry
        lax.fori_loop(0, tm, body, 0, unroll=8)

    @pl.when(jnp.logical_and(first, j == 0))
    def _():
        gather(tok_ref, hp_refs[0], xe_buf, _SEM_XE)

    @pl.when(j == 0)
    def _():
        wait_rows(hp_refs[0], xe_buf, _SEM_XE)

    @pl.when(j < nf)
    def _():
        half = xe_buf.shape[1]
        w = xe_buf[...]
        lo = pltpu.bitcast(w << 16, jnp.float32).astype(jnp.bfloat16)
        hi = pltpu.bitcast(w & jnp.uint32(0xFFFF0000), jnp.float32).astype(jnp.bfloat16)

        def proj(w_ref):
            return (jnp.dot(lo, w_ref[0, :half, :], preferred_element_type=jnp.float32)
                    + jnp.dot(hi, w_ref[0, half:, :], preferred_element_type=jnp.float32))

        g, u = proj(wg_ref), proj(wu_ref)
        hidden = ((g / (1.0 + jnp.exp(-g))) * u).astype(hid_buf.dtype)
        hid_buf[:, pl.ds(pl.multiple_of(j * tf, tf), tf)] = hidden

    @pl.when(j == nf // 2)
    def _():
        @pl.when(jnp.logical_not(first))
        def _():
            wait_rows(row_buf, x_refs[0], _SEM_OUT)
        per_group(group, lambda g: gather(tok_ref, x_refs[g], row_buf, _SEM_ROWS))

    @pl.when(j == nf)
    def _():
        @pl.when(jnp.logical_not(last))
        def _():
            per_group(group_next, lambda g: gather(tok_next_ref, hp_refs[g], xe_buf, _SEM_XE))
        wait_rows(x_refs[0], row_buf, _SEM_ROWS)

    @pl.when(j >= nf)
    def _():
        y = jnp.dot(hid_buf[...], wd_ref[0], preferred_element_type=jnp.float32)
        cols = pl.ds(pl.multiple_of((j - nf) * tn, tn), tn)
        row_buf[:, cols] += y * gate_ref[0]

    @pl.when(j == nf + nd - 1)
    def _():
        per_group(group, lambda g: scatter(tok_ref, row_buf, x_refs[g], _SEM_OUT))

        @pl.when(last)
        def _():
            wait_rows(row_buf, x_refs[0], _SEM_OUT)


def _experts(tok, gate, w_gate, w_up, w_down, hps, xs, tm=512, tf=512, tn=512):
    n_exp, t = tok.shape
    d = xs[0].shape[1]
    d_ff = w_gate.shape[-1]
    cap = t // N_GROUPS
    tm = min(tm, cap)
    assert cap % tm == 0 and d_ff % tf == 0 and d % tn == 0
    tiles = t // tm
    nf, nd = d_ff // tf, d // tn
    n_tiles = n_exp * tiles
    tok3 = tok.reshape(n_tiles, 1, tm)
    gate3 = gate.reshape(n_tiles, tm, 1)
    any_spec = pl.BlockSpec(memory_space=pl.ANY)
    kern = functools.partial(_expert_kernel, nf=nf, nd=nd, tm=tm, tf=tf, tn=tn, tiles=tiles,
                             tiles_per_group=cap // tm)
    return pl.pallas_call(
        kern,
        grid=(n_exp, tiles, nf + nd),
        in_specs=[pl.BlockSpec((1, 1, tm), lambda e, i, j: (e * tiles + i, 0, 0), memory_space=pltpu.SMEM),
                  pl.BlockSpec((1, 1, tm), lambda e, i, j: (jnp.minimum(e * tiles + i + 1, n_tiles - 1), 0, 0),
                               memory_space=pltpu.SMEM),
                  pl.BlockSpec((1, tm, 1), lambda e, i, j: (e * tiles + i, 0, 0)),
                  pl.BlockSpec((1, d, tf), lambda e, i, j: (e, 0, jnp.minimum(j, nf - 1))),
                  pl.BlockSpec((1, d, tf), lambda e, i, j: (e, 0, jnp.minimum(j, nf - 1))),
                  pl.BlockSpec((1, d_ff, tn), lambda e, i, j: (e, 0, jnp.maximum(j - nf, 0))),
                  any_spec, any_spec, any_spec, any_spec],
        out_specs=[any_spec, any_spec],
        out_shape=[jax.ShapeDtypeStruct(x.shape, x.dtype) for x in xs],
        input_output_aliases={8: 0, 9: 1},
        scratch_shapes=[pltpu.VMEM((tm, d // 2), jnp.uint32),
                        pltpu.VMEM((tm, d_ff), jnp.bfloat16),
                        pltpu.VMEM((tm, d), jnp.float32),
                        pltpu.SemaphoreType.DMA((3,))],
        compiler_params=_params(3),
        name="experts",
    )(tok3, tok3, gate3, w_gate, w_up, w_down, hps[0], hps[1], xs[0], xs[1])


def _mixer_layer(x, batch, seq, layer, attn_norm, w_in, diff_lambda, diff_subln, gqa_q_norm, gqa_k_norm,
                 na_rel_bias, w_out):
    h = _rmsnorm(x, attn_norm, jnp.bfloat16)
    proj = _matmul(h, w_in, jnp.bfloat16)
    mixed = (_fourier(proj, batch, seq),
             _diff_attn(proj, diff_lambda, diff_subln, layer, batch, seq),
             _gqa(proj, gqa_q_norm, gqa_k_norm, batch, seq),
             _na(proj, na_rel_bias, batch, seq))
    return _out_proj(mixed, w_out, x)


def _ffn_layer(xs, ffn_norm, w_router, w_gate, w_up, w_down):
    wr = jnp.pad(w_router, ((0, 0), (0, V7X_LANES - N_EXPERTS))).astype(jnp.bfloat16)
    assert len(xs) == N_GROUPS and xs[0].shape == xs[1].shape
    hps, gates, toks = [], [], []
    for x in xs:
        n = x.shape[0]
        cap = EC_CAPACITY * n // N_EXPERTS
        hp, aff = _norm_router(x, ffn_norm, wr)
        gate, tok = lax.top_k(aff[:, :N_EXPERTS].T, cap)
        hps.append(hp)
        gates.append(gate)
        toks.append(tok)
    return _experts(jnp.concatenate(toks, axis=1), jnp.concatenate(gates, axis=1),
                    w_gate, w_up, w_down, hps, xs)


def kernel(x_prompt, x_sample, attn_norm, w_in, diff_lambda, diff_subln, gqa_q_norm, gqa_k_norm, na_rel_bias,
           w_out, ffn_norm, w_router, w_gate, w_up, w_down, final_norm):
    shapes = [x_prompt.shape, x_sample.shape]
    xs = [x_prompt.reshape(-1, D_MODEL), x_sample.reshape(-1, D_MODEL)]
    w_in_b = w_in.astype(jnp.bfloat16)
    w_out_b = w_out.astype(jnp.bfloat16)
    for l in range(w_in.shape[0]):
        xs = [_mixer_layer(x, shp[0], shp[1], l, attn_norm[l], w_in_b[l], diff_lambda[l], diff_subln[l],
                           gqa_q_norm[l], gqa_k_norm[l], na_rel_bias[l], w_out_b[l])
              for x, shp in zip(xs, shapes)]
        xs = _ffn_layer(xs, ffn_norm[l], w_router[l], w_gate[l].astype(jnp.bfloat16),
                        w_up[l].astype(jnp.bfloat16), w_down[l].astype(jnp.bfloat16))
    return tuple(_rmsnorm(x, final_norm, jnp.float32).reshape(shp) for x, shp in zip(xs, shapes))
```
